```python
import math
import jax, jax.numpy as jnp
from jax import lax
import numpy as np

D_MODEL = 1024
BATCH = 4
SEQ = 4096
DEPTH = 2

MEM_LEN = 256
BRANCH_WIDTH = 512
N_BRANCHES = 5
GDN_HEADS = 4
GDN_HEAD_DIM = 128
GDN_CONV = 4
GDN_CHUNK = 64
CONF_WIDTH = 31
SWA_Q_HEADS = 8
SWA_KV_HEADS = 2
SWA_HEAD_DIM = 64
WINDOW = 128
REL_BUCKETS = 32
REL_MAX_DIST = 128
LRU_BLOCKS = 8
LRU_BLOCK_W = BRANCH_WIDTH // LRU_BLOCKS
LRU_CONV = 4
LRU_C = 8.0
XATTN_HEADS = 4
XATTN_HEAD_DIM = BRANCH_WIDTH // XATTN_HEADS
NORM_EPS = 1e-6

SPLIT_SIZES = (3 * BRANCH_WIDTH, BRANCH_WIDTH, GDN_HEADS, GDN_HEADS,
               2 * BRANCH_WIDTH, BRANCH_WIDTH,
               SWA_Q_HEADS * SWA_HEAD_DIM, 2 * SWA_KV_HEADS * SWA_HEAD_DIM, BRANCH_WIDTH,
               BRANCH_WIDTH, BRANCH_WIDTH,
               XATTN_HEADS * XATTN_HEAD_DIM, BRANCH_WIDTH,
               N_BRANCHES * D_MODEL)
IN_COLS = sum(SPLIT_SIZES)

kernel_name = "hybrid_gated_five_mixer_block"


def _rms_norm(x, g):
    xf = x.astype(jnp.float32)
    y = xf * lax.rsqrt(jnp.mean(xf * xf, axis=-1, keepdims=True) + NORM_EPS)
    return (y * g.astype(jnp.float32)).astype(x.dtype)


def _layer_norm(x, g, b):
    xf = x.astype(jnp.float32)
    mu = jnp.mean(xf, axis=-1, keepdims=True)
    var = jnp.mean(jnp.square(xf - mu), axis=-1, keepdims=True)
    y = (xf - mu) * lax.rsqrt(var + NORM_EPS)
    return (y * g.astype(jnp.float32) + b.astype(jnp.float32)).astype(x.dtype)


def _l2_norm(x):
    return x * lax.rsqrt(jnp.sum(x * x, axis=-1, keepdims=True) + NORM_EPS)


def _causal_dwconv(x, w, b=None):
    K, C = w.shape
    y = lax.conv_general_dilated(x, w[:, None, :].astype(x.dtype), window_strides=(1,),
                                 padding=((K - 1, 0),), dimension_numbers=('NWC', 'WIO', 'NWC'),
                                 feature_group_count=C)
    if b is not None:
        y = y + b.astype(x.dtype)
    return y


def _gated_delta_rule(q, k, v, g, beta):
    B, S, H, DK = q.shape
    DV = v.shape[-1]
    C = GDN_CHUNK
    N = S // C

    def chunks(t):
        return jnp.swapaxes(t.reshape((B, N, C, H) + t.shape[3:]), 2, 3)

    qc, kc, vc, gc, bc = (chunks(t) for t in (q, k, v, g, beta))
    G = jnp.cumsum(gc, axis=-1)
    idx = jnp.arange(C)
    tril = idx[:, None] >= idx[None, :]
    strict = idx[:, None] > idx[None, :]
    L = jnp.exp(jnp.where(tril, G[..., :, None] - G[..., None, :], -jnp.inf))
    kb = kc * bc[..., None]
    A = jnp.where(strict, jnp.einsum('bnhid,bnhjd->bnhij', kb, kc) * L, 0.0)
    rhs = jnp.concatenate([vc * bc[..., None], kb * jnp.exp(G)[..., None]], axis=-1)
    sol = lax.linalg.triangular_solve(A + jnp.eye(C, dtype=A.dtype), rhs, left_side=True, lower=True)
    u, w = sol[..., :DV], sol[..., DV:]
    attn = jnp.einsum('bnhid,bnhjd->bnhij', qc, kc) * L
    q_dec = qc * jnp.exp(G)[..., None]
    k_dec = kc * jnp.exp(G[..., -1:] - G)[..., None]
    chunk_decay = jnp.exp(G[..., -1])

    def step(state, inp):
        u_n, w_n, attn_n, qd_n, kd_n, cd_n = inp
        v_new = u_n - jnp.einsum('bhck,bhkv->bhcv', w_n, state)
        o = jnp.einsum('bhck,bhkv->bhcv', qd_n, state) + jnp.einsum('bhij,bhjv->bhiv', attn_n, v_new)
        state = state * cd_n[..., None, None] + jnp.einsum('bhck,bhcv->bhkv', kd_n, v_new)
        return state, o

    xs = tuple(jnp.moveaxis(t, 1, 0) for t in (u, w, attn, q_dec, k_dec, chunk_decay))
    s0 = jnp.zeros((B, H, DK, DV), q.dtype)
    _, o = lax.scan(step, s0, xs)
    o = jnp.moveaxis(o, 0, 1)
    return jnp.swapaxes(o, 2, 3).reshape(B, S, H, DV)


def _t5_bucket(dist):
    n = np.maximum(dist, 0)
    max_exact = REL_BUCKETS // 2
    large = max_exact + (np.log(np.maximum(n, 1) / max_exact) / np.log(REL_MAX_DIST / max_exact)
                         * (REL_BUCKETS - max_exact)).astype(np.int32)
    large = np.minimum(large, REL_BUCKETS - 1)
    return np.where(n < max_exact, n, large).astype(np.int32)


def _swa_sinks(q, k, v, sinks, band_bias):
    B, S, HQ, hd = q.shape
    HKV = k.shape[2]
    G = HQ // HKV
    W = WINDOW
    N = S // W
    qb = q.reshape(B, N, W, HKV, G, hd)

    def band(t):
        tb = t.reshape(B, N, W, HKV, hd)
        prev = jnp.pad(tb, ((0, 0), (1, 0), (0, 0), (0, 0), (0, 0)))[:, :-1]
        return jnp.concatenate([prev, tb], axis=2)

    kb, vb = band(k), band(v)
    s = jnp.einsum('bniKgd,bnjKd->bnKgij', qb, kb).astype(jnp.float32) * (hd ** -0.5)
    s = s + band_bias.astype(jnp.float32).reshape(1, 1, HKV, G, W, 2 * W)
    i = jnp.arange(W)[:, None]
    j = jnp.arange(2 * W)[None, :]
    dist = i + W - j
    valid = (dist >= 0) & (dist < W)
    valid = valid[None] & ((jnp.arange(N)[:, None, None] > 0) | (j >= W)[None])
    s = jnp.where(valid[None, :, None, None], s, -jnp.inf)
    sink = sinks.astype(jnp.float32).reshape(1, 1, HKV, G, 1, 1)
    m = jnp.maximum(jnp.max(s, axis=-1, keepdims=True), sink)
    p = jnp.exp(s - m)
    p = p / (jnp.sum(p, axis=-1, keepdims=True) + jnp.exp(sink - m))
    o = jnp.einsum('bnKgij,bnjKd->bniKgd', p.astype(v.dtype), vb)
    return o.reshape(B, S, HQ, hd)


def _rglru(x, w_a, b_a, w_x, b_x, lam):
    B, S, W = x.shape
    xf = x.astype(jnp.float32)
    xb = xf.reshape(B, S, LRU_BLOCKS, LRU_BLOCK_W)
    r = jax.nn.sigmoid(jnp.einsum('bsnc,ncd->bsnd', xb, w_a.astype(jnp.float32)).reshape(B, S, W) + b_a)
    ig = jax.nn.sigmoid(jnp.einsum('bsnc,ncd->bsnd', xb, w_x.astype(jnp.float32)).reshape(B, S, W) + b_x)
    log_a = -LRU_C * r * jax.nn.softplus(-lam.astype(jnp.float32))
    a = jnp.exp(log_a)
    bterm = jnp.sqrt(-jnp.expm1(2.0 * log_a)) * (ig * xf)

    def combine(left, right):
        a1, b1 = left
        a2, b2 = right
        return a1 * a2, a2 * b1 + b2

    _, h = lax.associative_scan(combine, (a, bterm), axis=1)
    return h.astype(x.dtype)


def _hybrid_mixer(h, mem, w_in, b_in, a_conv_w, a_log, a_dt_bias, a_norm_g,
                  b_dw_w, b_dw_b, b_ln_g, b_ln_b, c_sinks, band_bias,
                  d_conv_w, d_conv_b, d_w_a, d_b_a, d_w_x, d_b_x, d_lambda,
                  g_mem, w_mem_kv, w_br, w_out):
    B, S, _ = h.shape
    dt = h.dtype
    proj = h @ w_in + b_in
    offsets = [int(o) for o in np.cumsum(SPLIT_SIZES)[:-1]]
    (a_qkv, a_z, a_beta, a_alpha, b_glu, b_z, c_q, c_kv, c_z,
     d_x, d_z, e_q, e_z, gates) = jnp.split(proj, offsets, axis=-1)

    qkv = jax.nn.silu(_causal_dwconv(a_qkv, a_conv_w)).astype(jnp.float32)
    aq, ak, av = jnp.split(qkv, 3, axis=-1)
    aq = _l2_norm(aq.reshape(B, S, GDN_HEADS, GDN_HEAD_DIM)) * (GDN_HEAD_DIM ** -0.5)
    ak = _l2_norm(ak.reshape(B, S, GDN_HEADS, GDN_HEAD_DIM))
    av = av.reshape(B, S, GDN_HEADS, GDN_HEAD_DIM)
    beta = jax.nn.sigmoid(a_beta.astype(jnp.float32))
    g = -jnp.exp(a_log.astype(jnp.float32)) * jax.nn.softplus(a_alpha.astype(jnp.float32) + a_dt_bias)
    ao = _gated_delta_rule(aq, ak, av, g, beta)
    ao = _rms_norm(ao, a_norm_g).reshape(B, S, BRANCH_WIDTH).astype(dt)
    y_a = ao * jax.nn.silu(a_z)

    glu_a, glu_b = jnp.split(b_glu, 2, axis=-1)
    bu = glu_a * jax.nn.sigmoid(glu_b)
    bu = _causal_dwconv(bu, b_dw_w, b_dw_b)
    bu = jax.nn.silu(_layer_norm(bu, b_ln_g, b_ln_b))
    y_b = bu * jax.nn.silu(b_z)

    cq = c_q.reshape(B, S, SWA_Q_HEADS, SWA_HEAD_DIM)
    ck, cv = jnp.split(c_kv, 2, axis=-1)
    ck = ck.reshape(B, S, SWA_KV_HEADS, SWA_HEAD_DIM)
    cv = cv.reshape(B, S, SWA_KV_HEADS, SWA_HEAD_DIM)
    co = _swa_sinks(cq, ck, cv, c_sinks, band_bias).reshape(B, S, BRANCH_WIDTH)
    y_c = co * jax.nn.silu(c_z)

    dxc = _causal_dwconv(d_x, d_conv_w, d_conv_b)
    do = _rglru(dxc, d_w_a, d_b_a, d_w_x, d_b_x, d_lambda)
    y_d = do * jax.nn.silu(d_z)

    mem_kv = _rms_norm(mem, g_mem) @ w_mem_kv
    mk, mv = jnp.split(mem_kv, 2, axis=-1)
    mk = mk.reshape(B, MEM_LEN, XATTN_HEADS, XATTN_HEAD_DIM)
    mv = mv.reshape(B, MEM_LEN, XATTN_HEADS, XATTN_HEAD_DIM)
    eq = e_q.reshape(B, S, XATTN_HEADS, XATTN_HEAD_DIM)
    es = jnp.einsum('bshd,bmhd->bhsm', eq, mk).astype(jnp.float32) * (XATTN_HEAD_DIM ** -0.5)
    ep = jax.nn.softmax(es, axis=-1).astype(dt)
    eo = jnp.einsum('bhsm,bmhd->bshd', ep, mv).reshape(B, S, BRANCH_WIDTH)
    y_e = eo * jax.nn.silu(e_z)

    ys = jnp.stack([y_a, y_b.astype(dt), y_c.astype(dt), y_d.astype(dt), y_e.astype(dt)], axis=2)
    branch_out = jnp.einsum('bsnw,nwd->bsnd', ys, w_br)
    gate = jax.nn.sigmoid(gates.reshape(B, S, N_BRANCHES, D_MODEL))
    merged = jnp.sum(gate * branch_out, axis=2)
    return merged @ w_out


def setup_inputs(seed: int = 0) -> dict:
    key = jax.random.key(seed)
    ks = jax.random.split(key, 32)
    f32 = jnp.float32
    L = DEPTH
    W = BRANCH_WIDTH

    def nrm(k, shape, scale):
        return scale * jax.random.normal(k, shape, f32)

    def gain(k, shape):
        return 1.0 + 0.05 * jax.random.normal(k, shape, f32)

    x = nrm(ks[0], (BATCH, SEQ, D_MODEL), 1.0)
    mem = nrm(ks[1], (BATCH, MEM_LEN, D_MODEL), 1.0)
    g_pre = gain(ks[2], (L, D_MODEL))
    g_post = gain(ks[3], (L, D_MODEL))
    w_in = nrm(ks[4], (L, D_MODEL, IN_COLS), D_MODEL ** -0.5)
    b_in = nrm(ks[5], (L, IN_COLS), 0.02)
    a_conv_w = nrm(ks[6], (L, GDN_CONV, 3 * W), GDN_CONV ** -0.5)
    a_log = jnp.log(jax.random.uniform(ks[7], (L, GDN_HEADS), f32, 1.0, 16.0))
    dt0 = jnp.exp(jax.random.uniform(ks[8], (L, GDN_HEADS), f32, math.log(1e-3), math.log(0.1)))
    a_dt_bias = dt0 + jnp.log(-jnp.expm1(-dt0))
    a_norm_g = gain(ks[9], (L, GDN_HEAD_DIM))
    b_dw_w = nrm(ks[10], (L, CONF_WIDTH, W), CONF_WIDTH ** -0.5)
    b_dw_b = nrm(ks[11], (L, W), 0.02)
    b_ln_g = gain(ks[12], (L, W))
    b_ln_b = nrm(ks[13], (L, W), 0.02)
    c_sinks = nrm(ks[14], (L, SWA_Q_HEADS), 0.5)
    rel_bias = nrm(ks[15], (REL_BUCKETS, SWA_Q_HEADS), 0.5)
    d_conv_w = nrm(ks[16], (L, LRU_CONV, W), LRU_CONV ** -0.5)
    d_conv_b = nrm(ks[17], (L, W), 0.02)
    d_w_a = nrm(ks[18], (L, LRU_BLOCKS, LRU_BLOCK_W, LRU_BLOCK_W), LRU_BLOCK_W ** -0.5)
    d_b_a = nrm(ks[19], (L, W), 0.02)
    d_w_x = nrm(ks[20], (L, LRU_BLOCKS, LRU_BLOCK_W, LRU_BLOCK_W), LRU_BLOCK_W ** -0.5)
    d_b_x = nrm(ks[21], (L, W), 0.02)
    u = jax.random.uniform(ks[22], (L, W), f32, 0.9, 0.999)
    a0 = u ** (1.0 / LRU_C)
    d_lambda = jnp.log(a0) - jnp.log1p(-a0)
    g_mem = gain(ks[23], (L, D_MODEL))
    w_mem_kv = nrm(ks[24], (L, D_MODEL, 2 * W), D_MODEL ** -0.5)
    w_br = nrm(ks[25], (L, N_BRANCHES, W, D_MODEL), W ** -0.5)
    w_out = nrm(ks[26], (L, D_MODEL, D_MODEL), D_MODEL ** -0.5)
    return {"x": x, "mem": mem, "g_pre": g_pre, "g_post": g_post, "w_in": w_in, "b_in": b_in,
            "a_conv_w": a_conv_w, "a_log": a_log, "a_dt_bias": a_dt_bias, "a_norm_g": a_norm_g,
            "b_dw_w": b_dw_w, "b_dw_b": b_dw_b, "b_ln_g": b_ln_g, "b_ln_b": b_ln_b,
            "c_sinks": c_sinks, "rel_bias": rel_bias,
            "d_conv_w": d_conv_w, "d_conv_b": d_conv_b, "d_w_a": d_w_a, "d_b_a": d_b_a,
            "d_w_x": d_w_x, "d_b_x": d_b_x, "d_lambda": d_lambda,
            "g_mem": g_mem, "w_mem_kv": w_mem_kv, "w_br": w_br, "w_out": w_out}


def reference(x, mem, g_pre, g_post, w_in, b_in, a_conv_w, a_log, a_dt_bias, a_norm_g,
              b_dw_w, b_dw_b, b_ln_g, b_ln_b, c_sinks, rel_bias,
              d_conv_w, d_conv_b, d_w_a, d_b_a, d_w_x, d_b_x, d_lambda,
              g_mem, w_mem_kv, w_br, w_out):
    i = np.arange(WINDOW)[:, None]
    j = np.arange(2 * WINDOW)[None, :]
    bucket = _t5_bucket(i + WINDOW - j)
    band_bias = jnp.transpose(rel_bias[bucket], (2, 0, 1))
    for l in range(DEPTH):
        h = _rms_norm(x, g_pre[l])
        y = _hybrid_mixer(h, mem, w_in[l], b_in[l], a_conv_w[l], a_log[l], a_dt_bias[l], a_norm_g[l],
                          b_dw_w[l], b_dw_b[l], b_ln_g[l], b_ln_b[l], c_sinks[l], band_bias,
                          d_conv_w[l], d_conv_b[l], d_w_a[l], d_b_a[l], d_w_x[l], d_b_x[l], d_lambda[l],
                          g_mem[l], w_mem_kv[l], w_br[l], w_out[l])
        x = x + _rms_norm(y, g_post[l])
    return x
```

```python
import functools

import numpy as np
import jax
import jax.numpy as jnp
from jax import lax
from jax.experimental import pallas as pl
from jax.experimental.pallas import tpu as pltpu

F32 = jnp.float32
BF16 = jnp.bfloat16

D_MODEL = 1024
BRANCH_W = 512
N_BRANCHES = 5
GDN_HEADS = 4
GDN_HD = 128
GDN_CONV = 4
GDN_CHUNK = 64
CONF_WIDTH = 31
SWA_Q_HEADS = 8
SWA_KV_HEADS = 2
SWA_HD = 64
WINDOW = 128
REL_BUCKETS = 32
REL_MAX_DIST = 128
LRU_BLOCKS = 8
LRU_CONV = 4
LRU_C = 8.0
XATTN_HEADS = 4
XATTN_HD = 128
NORM_EPS = 1e-6

SPLIT_SIZES = (3 * BRANCH_W, BRANCH_W, GDN_HEADS, GDN_HEADS,
               2 * BRANCH_W, BRANCH_W,
               SWA_Q_HEADS * SWA_HD, 2 * SWA_KV_HEADS * SWA_HD, BRANCH_W,
               BRANCH_W, BRANCH_W,
               XATTN_HEADS * XATTN_HD, BRANCH_W,
               N_BRANCHES * D_MODEL)
_OFF = [0] + [int(o) for o in np.cumsum(SPLIT_SIZES)]
(O_AQKV, O_AZ, O_ABETA, O_AALPHA, O_BGLU, O_BZ, O_CQ, O_CKV, O_CZ,
 O_DX, O_DZ, O_EQ, O_EZ, O_GATES, O_END) = _OFF

SUBLANES = 8
LANES = 128
TOKEN_TILE = 512
VMEM_LIMIT = 56 * 1024 * 1024
MASK_VALUE = -1e30

_NT = (((1,), (1,)), ((), ()))
_TN = (((0,), (0,)), ((), ()))


def _sigmoid(x):
    return 1.0 / (1.0 + jnp.exp(-x))


def _silu(x):
    return x * _sigmoid(x)


def _softplus(x):
    return jnp.maximum(x, 0.0) + jnp.log1p(jnp.exp(-jnp.abs(x)))


def _bdot(a, b):
    return jnp.dot(a.astype(BF16), b.astype(BF16), preferred_element_type=F32)


def _rms(x, g):
    return x * lax.rsqrt(jnp.mean(x * x, axis=-1, keepdims=True) + NORM_EPS) * g


def _params(n_axes):
    return pltpu.CompilerParams(dimension_semantics=("arbitrary",) * n_axes,
                                vmem_limit_bytes=VMEM_LIMIT)


def _whole(arr):
    nd = arr.ndim
    return pl.BlockSpec(arr.shape, lambda *_: (0,) * nd)


def _prenorm_kernel(x_ref, g_ref, o_ref):
    o_ref[...] = _rms(x_ref[...], g_ref[...]).astype(o_ref.dtype)


def _prenorm(x2d, g):
    t, d = x2d.shape
    tm = min(t, 1024)
    return pl.pallas_call(
        _prenorm_kernel,
        out_shape=jax.ShapeDtypeStruct((t, d), BF16),
        grid=(t // tm,),
        in_specs=[pl.BlockSpec((tm, d), lambda i: (i, 0)), pl.BlockSpec((1, d), lambda i: (0, 0))],
        out_specs=pl.BlockSpec((tm, d), lambda i: (i, 0)),
        compiler_params=_params(1),
        name="prenorm",
    )(x2d, g.reshape(1, d))


def _t5_bucket_np(dist):
    n = np.maximum(dist, 0)
    max_exact = REL_BUCKETS // 2
    large = max_exact + (np.log(np.maximum(n, 1) / max_exact) / np.log(REL_MAX_DIST / max_exact)
                         * (REL_BUCKETS - max_exact)).astype(np.int32)
    large = np.minimum(large, REL_BUCKETS - 1)
    return np.where(n < max_exact, n, large).astype(np.int32)


def _band_bias_kernel(rel_ref, bucket_ref, o_ref):
    bucket = bucket_ref[...]
    i = lax.broadcasted_iota(jnp.int32, bucket.shape, 0)
    j = lax.broadcasted_iota(jnp.int32, bucket.shape, 1)
    dist = i + WINDOW - j
    valid = (dist >= 0) & (dist < WINDOW)
    for h in range(SWA_Q_HEADS):
        acc = jnp.zeros(bucket.shape, F32)
        for b in range(REL_BUCKETS):
            acc = jnp.where(bucket == b, rel_ref[b, h], acc)
        o_ref[h] = jnp.where(valid, acc, MASK_VALUE)


def _band_bias(rel_bias):
    i = np.arange(WINDOW)[:, None]
    j = np.arange(2 * WINDOW)[None, :]
    bucket = jnp.asarray(_t5_bucket_np(i + WINDOW - j))
    return pl.pallas_call(
        _band_bias_kernel,
        out_shape=jax.ShapeDtypeStruct((SWA_Q_HEADS, WINDOW, 2 * WINDOW), F32),
        in_specs=[pl.BlockSpec(memory_space=pltpu.SMEM), _whole(bucket)],
        out_specs=pl.BlockSpec((SWA_Q_HEADS, WINDOW, 2 * WINDOW), lambda: (0, 0, 0)),
        name="band_bias",
    )(rel_bias, bucket)


def _memkv_kernel(m_ref, g_ref, w_ref, o_ref):
    o_ref[0] = _bdot(_rms(m_ref[0], g_ref[...]), w_ref[...]).astype(o_ref.dtype)


def _memkv(mem, g, w):
    b, m, d = mem.shape
    n = w.shape[1]
    return pl.pallas_call(
        _memkv_kernel,
        out_shape=jax.ShapeDtypeStruct((b, m, n), BF16),
        grid=(b,),
        in_specs=[pl.BlockSpec((1, m, d), lambda i: (i, 0, 0)), pl.BlockSpec((1, d), lambda i: (0, 0)),
                  _whole(w)],
        out_specs=pl.BlockSpec((1, m, n), lambda i: (i, 0, 0)),
        compiler_params=_params(1),
        name="memkv",
    )(mem, g.reshape(1, d), w)


def _mixer_call(kernel, name, h, batch, consts, scratch, smem_consts=()):
    t, d = h.shape
    seq = t // batch
    ts = min(TOKEN_TILE, seq)
    ns = seq // ts
    tile = pl.BlockSpec((ts, d), lambda b, s: (b * ns + s, 0))
    in_specs, args = [tile], [h]
    for c in consts:
        spec, arr = c if isinstance(c, tuple) else (_whole(c), c)
        in_specs.append(spec)
        args.append(arr)
    in_specs += [pl.BlockSpec(memory_space=pltpu.SMEM) for _ in smem_consts]
    args += list(smem_consts)
    return pl.pallas_call(
        kernel,
        out_shape=jax.ShapeDtypeStruct((t, d), BF16),
        grid=(batch, ns),
        in_specs=in_specs,
        out_specs=tile,
        scratch_shapes=scratch,
        compiler_params=_params(2),
        name=name,
    )(*args)


def _proj(h, w_ref, b_ref, lo, hi):
    return jnp.dot(h, w_ref[:, lo:hi], preferred_element_type=F32) + b_ref[:, lo:hi]


def _causal_conv(xpad_ref, w_ref, ts, halo, width):
    acc = None
    for k in range(width):
        term = w_ref[k:k + 1, :] * xpad_ref[halo - width + 1 + k: halo - width + 1 + k + ts, :]
        acc = term if acc is None else acc + term
    return acc


def _lru_kernel(h_ref, w_ref, b_ref, cw_ref, cb_ref, wa_ref, ba_ref, wx_ref, bx_ref, lam_ref, wbr_ref,
                o_ref, xpad_ref, a_ref, bt_ref, hs_ref, hc_ref):
    ts = h_ref.shape[0]
    w = BRANCH_W
    halo = SUBLANES

    @pl.when(pl.program_id(1) == 0)
    def _():
        xpad_ref[0:halo, :] = jnp.zeros((halo, w), F32)
        hc_ref[...] = jnp.zeros_like(hc_ref)

    h = h_ref[...]
    xpad_ref[halo:halo + ts, :] = _proj(h, w_ref, b_ref, 0, w)
    dxc = _causal_conv(xpad_ref, cw_ref, ts, halo, LRU_CONV) + cb_ref[...]
    xpad_ref[0:halo, :] = xpad_ref[ts:ts + halo, :]

    r = _sigmoid(_bdot(dxc, wa_ref[...]) + ba_ref[...])
    ig = _sigmoid(_bdot(dxc, wx_ref[...]) + bx_ref[...])
    log_a = -LRU_C * r * _softplus(-lam_ref[...])
    a = jnp.exp(log_a)
    bt = jnp.sqrt(-jnp.tanh(log_a) * (a * a + 1.0)) * (ig * dxc)

    row = lax.broadcasted_iota(jnp.int32, (ts, w), 0) & (SUBLANES - 1)
    for sh in (1, 2, 4):
        m = row >= sh
        bt = jnp.where(m, a * pltpu.roll(bt, sh, 0) + bt, bt)
        a = jnp.where(m, a * pltpu.roll(a, sh, 0), a)
    a_ref[...] = a
    bt_ref[...] = bt

    def group(g, carry):
        r0 = pl.multiple_of(g * SUBLANES, SUBLANES)
        hg = a_ref[pl.ds(r0, SUBLANES), :] * carry + bt_ref[pl.ds(r0, SUBLANES), :]
        hs_ref[pl.ds(r0, SUBLANES), :] = hg
        return hg[SUBLANES - 1:SUBLANES, :]

    hc_ref[0:1, :] = lax.fori_loop(0, ts // SUBLANES, group, hc_ref[0:1, :], unroll=8)

    y = hs_ref[...] * _silu(_proj(h, w_ref, b_ref, w, 2 * w))
    gate = _sigmoid(_proj(h, w_ref, b_ref, 2 * w, 2 * w + D_MODEL))
    o_ref[...] = (gate * _bdot(y, wbr_ref[...])).astype(o_ref.dtype)


CONF_HALO = 32


def _conf_kernel(h_ref, w_ref, b_ref, dw_ref, dwb_ref, lng_ref, lnb_ref, wbr_ref, o_ref, xpad_ref):
    ts = h_ref.shape[0]
    w = BRANCH_W

    @pl.when(pl.program_id(1) == 0)
    def _():
        xpad_ref[0:CONF_HALO, :] = jnp.zeros((CONF_HALO, w), F32)

    h = h_ref[...]
    xpad_ref[CONF_HALO:CONF_HALO + ts, :] = (_proj(h, w_ref, b_ref, 0, w)
                                             * _sigmoid(_proj(h, w_ref, b_ref, w, 2 * w)))
    bu = _causal_conv(xpad_ref, dw_ref, ts, CONF_HALO, CONF_WIDTH) + dwb_ref[...]
    xpad_ref[0:CONF_HALO, :] = xpad_ref[ts:ts + CONF_HALO, :]

    mu = jnp.mean(bu, axis=-1, keepdims=True)
    cen = bu - mu
    var = jnp.mean(cen * cen, axis=-1, keepdims=True)
    ln = cen * lax.rsqrt(var + NORM_EPS) * lng_ref[...] + lnb_ref[...]
    y = _silu(ln) * _silu(_proj(h, w_ref, b_ref, 2 * w, 3 * w))
    gate = _sigmoid(_proj(h, w_ref, b_ref, 3 * w, 3 * w + D_MODEL))
    o_ref[...] = (gate * _bdot(y, wbr_ref[...])).astype(o_ref.dtype)


def _xattn_kernel(h_ref, w_ref, b_ref, kv_ref, wbr_ref, o_ref):
    w = BRANCH_W
    h = h_ref[...]
    q = _proj(h, w_ref, b_ref, 0, w).astype(BF16)
    kv = kv_ref[0]
    heads = []
    for hd in range(XATTN_HEADS):
        lo = hd * XATTN_HD
        s = lax.dot_general(q[:, lo:lo + XATTN_HD], kv[:, lo:lo + XATTN_HD], _NT,
                            preferred_element_type=F32) * (XATTN_HD ** -0.5)
        p = jnp.exp(s - jnp.max(s, axis=-1, keepdims=True))
        heads.append(_bdot(p, kv[:, w + lo:w + lo + XATTN_HD]) / jnp.sum(p, axis=-1, keepdims=True))
    y = jnp.concatenate(heads, axis=1) * _silu(_proj(h, w_ref, b_ref, w, 2 * w))
    gate = _sigmoid(_proj(h, w_ref, b_ref, 2 * w, 2 * w + D_MODEL))
    o_ref[...] = (gate * _bdot(y, wbr_ref[...])).astype(o_ref.dtype)


def _swa_kernel(h_ref, w_ref, b_ref, bias_ref, wbr_ref, sinks_ref, o_ref, kv_ref):
    ts = h_ref.shape[0]
    w = BRANCH_W
    kvw = 2 * SWA_KV_HEADS * SWA_HD
    group = SWA_Q_HEADS // SWA_KV_HEADS
    first_tile = pl.program_id(1) == 0

    @pl.when(first_tile)
    def _():
        kv_ref[0:WINDOW, :] = jnp.zeros((WINDOW, kvw), F32)

    h = h_ref[...]
    q = _proj(h, w_ref, b_ref, 0, w).astype(BF16)
    kv_ref[WINDOW:WINDOW + ts, :] = _proj(h, w_ref, b_ref, w, w + kvw)
    in_prev = lax.broadcasted_iota(jnp.int32, (WINDOW, 2 * WINDOW), 1) < WINDOW

    blocks = []
    for blk in range(ts // WINDOW):
        band = kv_ref[blk * WINDOW:(blk + 2) * WINDOW, :].astype(BF16)
        qb = q[blk * WINDOW:(blk + 1) * WINDOW, :]
        heads = []
        for hq in range(SWA_Q_HEADS):
            kvh = hq // group
            k = band[:, kvh * SWA_HD:(kvh + 1) * SWA_HD]
            v = band[:, kvw // 2 + kvh * SWA_HD: kvw // 2 + (kvh + 1) * SWA_HD]
            s = lax.dot_general(qb[:, hq * SWA_HD:(hq + 1) * SWA_HD], k, _NT,
                                preferred_element_type=F32) * (SWA_HD ** -0.5) + bias_ref[hq]
            if blk == 0:
                s = jnp.where(first_tile & in_prev, MASK_VALUE, s)
            sink = sinks_ref[hq]
            m = jnp.maximum(jnp.max(s, axis=-1, keepdims=True), sink)
            p = jnp.exp(s - m)
            den = jnp.sum(p, axis=-1, keepdims=True) + jnp.exp(sink - m)
            heads.append(_bdot(p, v) / den)
        blocks.append(jnp.concatenate(heads, axis=1))
    kv_ref[0:WINDOW, :] = kv_ref[ts:ts + WINDOW, :]

    y = jnp.concatenate(blocks, axis=0) * _silu(_proj(h, w_ref, b_ref, w + kvw, 2 * w + kvw))
    gate = _sigmoid(_proj(h, w_ref, b_ref, 2 * w + kvw, 2 * w + kvw + D_MODEL))
    o_ref[...] = (gate * _bdot(y, wbr_ref[...])).astype(o_ref.dtype)


GDN_QKV = 3 * BRANCH_W
GDN_COLS = GDN_QKV + BRANCH_W + D_MODEL + LANES


def _chunk_cumsum(x, axis):
    pos = lax.broadcasted_iota(jnp.int32, x.shape, axis) & (GDN_CHUNK - 1)
    sh = 1
    while sh < GDN_CHUNK:
        x = x + jnp.where(pos >= sh, pltpu.roll(x, sh, axis), 0.0)
        sh *= 2
    return x


def _gdn_kernel(h_ref, w_ref, b_ref, wba_t_ref, bba_t_ref, cw_ref, alog_r_ref, dt_r_ref, alog_c_ref,
                dt_c_ref, ng_ref, wbr_ref, o_ref, xpad_ref, state_ref, y_ref):
    ts = h_ref.shape[0]
    c = GDN_CHUNK
    hd_w = GDN_HD
    halo = SUBLANES

    @pl.when(pl.program_id(1) == 0)
    def _():
        xpad_ref[0:halo, :] = jnp.zeros((halo, GDN_QKV), F32)
        state_ref[...] = jnp.zeros_like(state_ref)

    h = h_ref[...]
    xpad_ref[halo:halo + ts, :] = _proj(h, w_ref, b_ref, 0, GDN_QKV)
    qkv = _silu(_causal_conv(xpad_ref, cw_ref, ts, halo, GDN_CONV))
    xpad_ref[0:halo, :] = xpad_ref[ts:ts + halo, :]

    ba = _proj(h, w_ref, b_ref, GDN_COLS - LANES, GDN_COLS)
    beta_c = _sigmoid(ba)
    g_c = _chunk_cumsum(-jnp.exp(alog_r_ref[...]) * _softplus(ba + dt_r_ref[...]), 0)
    ba_t = lax.dot_general(wba_t_ref[...], h, _NT, preferred_element_type=F32) + bba_t_ref[...]
    g_r = _chunk_cumsum(-jnp.exp(alog_c_ref[...]) * _softplus(ba_t + dt_c_ref[...]), 1)

    ii = lax.broadcasted_iota(jnp.int32, (c, c), 0)
    jj = lax.broadcasted_iota(jnp.int32, (c, c), 1)
    tril = ii >= jj
    strict = ii > jj

    def l2n(x):
        return x * lax.rsqrt(jnp.sum(x * x, axis=-1, keepdims=True) + NORM_EPS)

    states = [state_ref[hd] for hd in range(GDN_HEADS)]
    for ci in range(ts // c):
        r0 = ci * c
        for hd in range(GDN_HEADS):
            lo = hd * hd_w
            q = l2n(qkv[r0:r0 + c, lo:lo + hd_w]) * (hd_w ** -0.5)
            k = l2n(qkv[r0:r0 + c, BRANCH_W + lo:BRANCH_W + lo + hd_w])
            v = qkv[r0:r0 + c, 2 * BRANCH_W + lo:2 * BRANCH_W + lo + hd_w]
            bcol = beta_c[r0:r0 + c, hd:hd + 1]
            gcol = g_c[r0:r0 + c, GDN_HEADS + hd:GDN_HEADS + hd + 1]
            grow = g_r[GDN_HEADS + hd:GDN_HEADS + hd + 1, r0:r0 + c]
            decay = jnp.exp(jnp.where(tril, gcol - grow, MASK_VALUE))
            kb = k * bcol
            st = lax.dot_general(jnp.concatenate([kb, q], axis=0).astype(BF16), k.astype(BF16), _NT,
                                 preferred_element_type=F32)
            attn = st[c:] * decay
            eg = jnp.exp(gcol)
            rhs = jnp.concatenate([v * bcol, kb * eg], axis=1)
            m = jnp.where(strict, -(st[:c] * decay), 0.0)
            p = m
            for _ in range(5):
                m = _bdot(m, m)
                p = p + m + _bdot(p, m)
            sol = rhs + _bdot(p, rhs)
            u = sol[:, :hd_w]
            wk = sol[:, hd_w:]
            glast = gcol[c - 1:c, :]
            ws = _bdot(jnp.concatenate([wk, q * eg], axis=0), states[hd])
            v_new = u - ws[:c]
            o = ws[c:] + _bdot(attn, v_new)
            k_dec = k * jnp.exp(glast - gcol)
            states[hd] = states[hd] * jnp.exp(glast) + lax.dot_general(
                k_dec.astype(BF16), v_new.astype(BF16), _TN, preferred_element_type=F32)
            y_ref[r0:r0 + c, lo:lo + hd_w] = _rms(o, ng_ref[...])
    for hd in range(GDN_HEADS):
        state_ref[hd] = states[hd]

    y = y_ref[...] * _silu(_proj(h, w_ref, b_ref, GDN_QKV, GDN_QKV + BRANCH_W))
    gate = _sigmoid(_proj(h, w_ref, b_ref, GDN_QKV + BRANCH_W, GDN_QKV + BRANCH_W + D_MODEL))
    o_ref[...] = (gate * _bdot(y, wbr_ref[...])).astype(o_ref.dtype)


def _merge_kernel(emit_next, x_ref, t0_ref, t1_ref, t2_ref, t3_ref, t4_ref, wout_ref, gpost_ref, *rest):
    merged = (t0_ref[...].astype(F32) + t1_ref[...].astype(F32) + t2_ref[...].astype(F32)
              + t3_ref[...].astype(F32) + t4_ref[...].astype(F32))
    xn = x_ref[...] + _rms(_bdot(merged, wout_ref[...]), gpost_ref[...])
    if emit_next:
        gnext_ref, xo_ref, ho_ref = rest
        ho_ref[...] = _rms(xn, gnext_ref[...]).astype(ho_ref.dtype)
    else:
        (xo_ref,) = rest
    xo_ref[...] = xn


def _merge(x2d, terms, wout, gpost, gnext):
    t, d = x2d.shape
    tm = min(t, 512)
    tile = pl.BlockSpec((tm, d), lambda i: (i, 0))
    row = pl.BlockSpec((1, d), lambda i: (0, 0))
    emit_next = gnext is not None
    in_specs = [tile] * 6 + [_whole(wout), row] + ([row] if emit_next else [])
    args = [x2d, *terms, wout, gpost.reshape(1, d)] + ([gnext.reshape(1, d)] if emit_next else [])
    out_shape = [jax.ShapeDtypeStruct((t, d), F32)] + ([jax.ShapeDtypeStruct((t, d), BF16)] if emit_next else [])
    out = pl.pallas_call(
        functools.partial(_merge_kernel, emit_next),
        out_shape=out_shape,
        grid=(t // tm,),
        in_specs=in_specs,
        out_specs=[tile] * len(out_shape),
        compiler_params=_params(1),
        name="merge",
    )(*args)
    return (out[0], out[1]) if emit_next else (out[0], None)


def _cols(w, b, pieces, pad=0):
    wc = jnp.concatenate([w[:, lo:hi] for lo, hi in pieces], axis=1)
    bc = jnp.concatenate([b[lo:hi] for lo, hi in pieces])
    if pad:
        wc = jnp.pad(wc, ((0, 0), (0, pad)))
        bc = jnp.pad(bc, (0, pad))
    return wc.astype(BF16), bc.reshape(1, -1)


def _gate(n):
    return (O_GATES + n * D_MODEL, O_GATES + (n + 1) * D_MODEL)


def _block_diag(w):
    n, c, d = w.shape
    return (jnp.eye(n, dtype=w.dtype)[:, None, :, None] * w[:, :, None, :]).reshape(n * c, n * d)


def _pad_lanes(v, offset):
    return jnp.zeros((1, LANES), F32).at[0, offset:offset + v.shape[0]].set(v)


def _pad_rows(v, offset):
    return jnp.zeros((SUBLANES, 1), F32).at[offset:offset + v.shape[0], 0].set(v)


def kernel(x, mem, g_pre, g_post, w_in, b_in, a_conv_w, a_log, a_dt_bias, a_norm_g, b_dw_w, b_dw_b, b_ln_g,
           b_ln_b, c_sinks, rel_bias, d_conv_w, d_conv_b, d_w_a, d_b_a, d_w_x, d_b_x, d_lambda, g_mem,
           w_mem_kv, w_br, w_out):
    batch, seq, d = x.shape
    depth = w_in.shape[0]
    ts = min(TOKEN_TILE, seq)
    w = BRANCH_W
    row = lambda v: v.reshape(1, -1)

    band_bias = _band_bias(rel_bias)
    x2d = x.reshape(batch * seq, d)
    h = _prenorm(x2d, g_pre[0])
    for l in range(depth):
        wl, bl = w_in[l], b_in[l]
        wbr = w_br[l].astype(BF16)

        wa, ba = _cols(wl, bl, [(O_AQKV, O_ABETA), _gate(0), (O_ABETA, O_BGLU)], pad=LANES - 2 * GDN_HEADS)
        wba_t = wl[:, O_ABETA:O_BGLU].T.astype(BF16)
        bba_t = bl[O_ABETA:O_BGLU].reshape(-1, 1)
        term_a = _mixer_call(
            _gdn_kernel, "gdn", h, batch,
            [wa, ba, wba_t, bba_t, a_conv_w[l], _pad_lanes(a_log[l], GDN_HEADS),
             _pad_lanes(a_dt_bias[l], GDN_HEADS), _pad_rows(a_log[l], GDN_HEADS),
             _pad_rows(a_dt_bias[l], GDN_HEADS), row(a_norm_g[l]), wbr[0]],
            [pltpu.VMEM((ts + SUBLANES, GDN_QKV), F32), pltpu.VMEM((GDN_HEADS, GDN_HD, GDN_HD), F32),
             pltpu.VMEM((ts, w), F32)])

        wb, bb = _cols(wl, bl, [(O_BGLU, O_CQ), _gate(1)])
        term_b = _mixer_call(
            _conf_kernel, "conformer", h, batch,
            [wb, bb, b_dw_w[l], row(b_dw_b[l]), row(b_ln_g[l]), row(b_ln_b[l]), wbr[1]],
            [pltpu.VMEM((ts + CONF_HALO, w), F32)])

        wc, bc = _cols(wl, bl, [(O_CQ, O_DX), _gate(2)])
        term_c = _mixer_call(
            _swa_kernel, "swa", h, batch, [wc, bc, band_bias, wbr[2]],
            [pltpu.VMEM((ts + WINDOW, 2 * SWA_KV_HEADS * SWA_HD), F32)], smem_consts=[c_sinks[l]])

        wd, bd = _cols(wl, bl, [(O_DX, O_EQ), _gate(3)])
        term_d = _mixer_call(
            _lru_kernel, "rglru", h, batch,
            [wd, bd, d_conv_w[l], row(d_conv_b[l]), _block_diag(d_w_a[l]).astype(BF16), row(d_b_a[l]),
             _block_diag(d_w_x[l]).astype(BF16), row(d_b_x[l]), row(d_lambda[l]), wbr[3]],
            [pltpu.VMEM((ts + SUBLANES, w), F32), pltpu.VMEM((ts, w), F32), pltpu.VMEM((ts, w), F32),
             pltpu.VMEM((ts, w), F32), pltpu.VMEM((SUBLANES, w), F32)])

        we, be = _cols(wl, bl, [(O_EQ, O_GATES), _gate(4)])
        mem_kv = _memkv(mem, g_mem[l], w_mem_kv[l].astype(BF16))
        ns = seq // ts
        kv_spec = pl.BlockSpec((1,) + mem_kv.shape[1:], lambda b, s: (b, 0, 0))
        term_e = _mixer_call(_xattn_kernel, "xattn", h, batch, [we, be, (kv_spec, mem_kv), wbr[4]], [])

        x2d, h = _merge(x2d, [term_a, term_b, term_c, term_d, term_e], w_out[l].astype(BF16), g_post[l],
                        g_pre[l + 1] if l + 1 < depth else None)
    return x2d.reshape(batch, seq, d)
```
